```python
import jax, jax.numpy as jnp
from jax import lax
import numpy as np

D_MODEL = 2048
BATCH = 4
SEQ = 2048
DEPTH = 1
DEC_BATCH = 1
DEC_SEQ = 8192
PAST_LEN = 128

CHUNK = 64
HG_HEADS = 8
HG_DK = 128
HG_DV = 128
HG_WIDTH = HG_HEADS * HG_DK
GLA_HEADS = 4
GLA_DK = 128
GLA_DV = 256
GLA_KW = GLA_HEADS * GLA_DK
GLA_VW = GLA_HEADS * GLA_DV
GLA_GATE_RANK = 16
GLA_GATE_TEMP = 16.0
N_EXPERTS = 16
EXPERT_FF = 2048
CAPACITY_FACTOR = 2
EPS = 1e-6

SPLIT_SIZES = (
    HG_WIDTH,
    HG_WIDTH,
    HG_WIDTH,
    HG_HEADS * HG_DV,
    HG_HEADS * HG_DV,
    GLA_KW,
    GLA_KW,
    GLA_VW,
    GLA_GATE_RANK,
    GLA_GATE_RANK,
    GLA_VW,
    D_MODEL,
    D_MODEL,
)
IN_WIDTH = sum(SPLIT_SIZES)

kernel_name = "hgrn2_gla_ec_bidir_encoder"


def rms_norm(x, w):
    xf = x.astype(jnp.float32)
    y = xf * lax.rsqrt(jnp.mean(xf * xf, axis=-1, keepdims=True) + EPS)
    return (y * w.astype(jnp.float32)).astype(x.dtype)


def head_rms_norm(o, w):
    b, t, h, d = o.shape
    y = o * lax.rsqrt(jnp.mean(o * o, axis=-1, keepdims=True) + EPS)
    return y.reshape(b, t, h * d) * w.astype(jnp.float32)


def chunked_gated_scan(q, k, v, log_f):
    bsz, t, h, dk = q.shape
    dv = v.shape[-1]
    n = t // CHUNK

    def to_chunks(a):
        return a.astype(jnp.float32).reshape(bsz, n, CHUNK, h, a.shape[-1]).transpose(1, 0, 3, 2, 4)

    incl = jnp.tril(jnp.ones((CHUNK, CHUNK), dtype=bool))[None, None, :, :, None]

    def step(state, inp):
        qc, kc, vc, gc = inp
        cum = jnp.cumsum(gc, axis=2)
        diff = cum[:, :, :, None, :] - cum[:, :, None, :, :]
        decay = jnp.exp(jnp.where(incl, diff, -jnp.inf))
        scores = jnp.einsum('bhik,bhjk,bhijk->bhij', qc, kc, decay)
        o = (jnp.einsum('bhij,bhjv->bhiv', scores, vc)
             + jnp.einsum('bhik,bhkv->bhiv', qc * jnp.exp(cum), state))
        last = cum[:, :, -1:, :]
        new_state = (jnp.exp(last[:, :, 0, :])[..., None] * state
                     + jnp.einsum('bhjk,bhjv->bhkv', kc * jnp.exp(last - cum), vc))
        return new_state, o

    s0 = jnp.zeros((bsz, h, dk, dv), jnp.float32)
    _, o = lax.scan(step, s0, (to_chunks(q), to_chunks(k), to_chunks(v), to_chunks(log_f)))
    return o.transpose(1, 0, 3, 2, 4).reshape(bsz, t, h, dv)


def bidir_scan(q, v, k_fwd, g_fwd, k_bwd, g_bwd):
    flip = lambda a: jnp.flip(a, axis=1)
    fwd = chunked_gated_scan(q, k_fwd, v, g_fwd)
    bwd = flip(chunked_gated_scan(flip(q), flip(k_bwd), flip(v), flip(g_bwd)))
    return fwd + bwd


def hgrn2_branch(hq, hf_fwd, hf_bwd, hi, hg, lb, norm_w):
    b, t, _ = hq.shape
    heads = lambda a, d: a.reshape(b, t, HG_HEADS, d)

    def forget(z, low):
        f = low + (1.0 - low) * jax.nn.sigmoid(z.astype(jnp.float32))
        return heads(1.0 - f, HG_DK), heads(jnp.log(f), HG_DK)

    k_f, g_f = forget(hf_fwd, lb[0])
    k_b, g_b = forget(hf_bwd, lb[1])
    q = heads(jax.nn.silu(hq.astype(jnp.float32)), HG_DK)
    o = bidir_scan(q, heads(hi, HG_DV), k_f, g_f, k_b, g_b)
    return head_rms_norm(o, norm_w) * jax.nn.silu(hg.astype(jnp.float32))


def gla_branch(gq, gk, gv, ga_fwd, ga_bwd, gr, gate_a2, gate_bias, norm_w):
    b, t, _ = gq.shape
    heads = lambda a, d: a.reshape(b, t, GLA_HEADS, d)

    def log_alpha(a_low, w2, bias):
        z = (a_low @ w2 + bias).astype(jnp.float32)
        return heads(jax.nn.log_sigmoid(z) / GLA_GATE_TEMP, GLA_DK)

    g_f = log_alpha(ga_fwd, gate_a2[0], gate_bias[0])
    g_b = log_alpha(ga_bwd, gate_a2[1], gate_bias[1])
    q = heads(gq.astype(jnp.float32) * (GLA_DK ** -0.5), GLA_DK)
    k = heads(gk, GLA_DK)
    o = bidir_scan(q, heads(gv, GLA_DV), k, g_f, k, g_b)
    return head_rms_norm(o, norm_w) * jax.nn.silu(gr.astype(jnp.float32))


def expert_choice_ffn(x, w_router, w_gate, w_up, w_down):
    b, t, d = x.shape
    n = b * t
    cap = max(1, CAPACITY_FACTOR * n // N_EXPERTS)
    xt = x.reshape(n, d)
    aff = jax.nn.softmax((xt @ w_router).astype(jnp.float32), axis=-1)
    gval, idx = lax.top_k(aff.T, cap)
    xe = xt[idx]
    h = jax.nn.silu(jnp.einsum('ecd,edf->ecf', xe, w_gate)) * jnp.einsum('ecd,edf->ecf', xe, w_up)
    ye = jnp.einsum('ecf,efd->ecd', h, w_down) * gval[..., None].astype(x.dtype)
    y = jnp.zeros((n, d), x.dtype).at[idx.reshape(-1)].add(ye.reshape(-1, d))
    return y.reshape(b, t, d)


def trunk(x, ln1_w, w_in, hg_lb_logits, hg_norm_w, gla_gate_a2, gla_gate_bias, gla_norm_w,
          w_br_hg, w_br_gla, w_out, ln2_w, w_router, w_e_gate, w_e_up, w_e_down, lnf_w):
    lb_all = jnp.cumsum(jax.nn.softmax(hg_lb_logits.astype(jnp.float32), axis=1), axis=1)
    offsets = []
    acc = 0
    for s in SPLIT_SIZES[:-1]:
        acc += s
        offsets.append(acc)
    for l in range(DEPTH):
        xn = rms_norm(x, ln1_w[l])
        proj = xn @ w_in[l]
        (hq, hf_f, hf_b, hi, hg, gq, gk, gv, ga_f, ga_b, gr,
         m_hg, m_gla) = jnp.split(proj, offsets, axis=-1)
        o_hg = hgrn2_branch(hq, hf_f, hf_b, hi, hg, lb_all[:, l], hg_norm_w[l]).astype(x.dtype)
        o_gla = gla_branch(gq, gk, gv, ga_f, ga_b, gr, gla_gate_a2[l], gla_gate_bias[l],
                           gla_norm_w[l]).astype(x.dtype)
        merged = (jax.nn.sigmoid(m_hg) * (o_hg @ w_br_hg[l])
                  + jax.nn.sigmoid(m_gla) * (o_gla @ w_br_gla[l]))
        x = x + merged @ w_out[l]
        x = x + expert_choice_ffn(rms_norm(x, ln2_w[l]), w_router[l], w_e_gate[l], w_e_up[l], w_e_down[l])
    return rms_norm(x, lnf_w)


def setup_inputs(seed: int = 0) -> dict:
    key = jax.random.key(seed)
    ks = jax.random.split(key, 20)
    nrm = lambda k, shape, scale: jax.random.normal(k, shape, jnp.float32) * scale
    return {
        "x_prompt": nrm(ks[0], (BATCH, SEQ, D_MODEL), 1.0),
        "x_sample": nrm(ks[1], (DEC_BATCH, DEC_SEQ, D_MODEL), 1.0),
        "ln1_w": 1.0 + nrm(ks[2], (DEPTH, D_MODEL), 0.02),
        "w_in": nrm(ks[3], (DEPTH, D_MODEL, IN_WIDTH), D_MODEL ** -0.5),
        "hg_lb_logits": nrm(ks[4], (2, DEPTH + 1, HG_WIDTH), 0.1),
        "hg_norm_w": 1.0 + nrm(ks[5], (DEPTH, HG_HEADS * HG_DV), 0.02),
        "gla_gate_a2": nrm(ks[6], (DEPTH, 2, GLA_GATE_RANK, GLA_KW), GLA_GATE_RANK ** -0.5),
        "gla_gate_bias": nrm(ks[7], (DEPTH, 2, GLA_KW), 0.1),
        "gla_norm_w": 1.0 + nrm(ks[8], (DEPTH, GLA_VW), 0.02),
        "w_br_hg": nrm(ks[9], (DEPTH, HG_HEADS * HG_DV, D_MODEL), (HG_HEADS * HG_DV) ** -0.5),
        "w_br_gla": nrm(ks[10], (DEPTH, GLA_VW, D_MODEL), GLA_VW ** -0.5),
        "w_out": nrm(ks[11], (DEPTH, D_MODEL, D_MODEL), D_MODEL ** -0.5),
        "ln2_w": 1.0 + nrm(ks[12], (DEPTH, D_MODEL), 0.02),
        "w_router": nrm(ks[13], (DEPTH, D_MODEL, N_EXPERTS), D_MODEL ** -0.5),
        "w_e_gate": nrm(ks[14], (DEPTH, N_EXPERTS, D_MODEL, EXPERT_FF), D_MODEL ** -0.5),
        "w_e_up": nrm(ks[15], (DEPTH, N_EXPERTS, D_MODEL, EXPERT_FF), D_MODEL ** -0.5),
        "w_e_down": nrm(ks[16], (DEPTH, N_EXPERTS, EXPERT_FF, D_MODEL), EXPERT_FF ** -0.5),
        "lnf_w": 1.0 + nrm(ks[17], (D_MODEL,), 0.02),
    }


def reference(x_prompt, x_sample, ln1_w, w_in, hg_lb_logits, hg_norm_w, gla_gate_a2, gla_gate_bias,
              gla_norm_w, w_br_hg, w_br_gla, w_out, ln2_w, w_router, w_e_gate, w_e_up, w_e_down, lnf_w):
    y_prompt = trunk(x_prompt, ln1_w, w_in, hg_lb_logits, hg_norm_w, gla_gate_a2, gla_gate_bias,
                     gla_norm_w, w_br_hg, w_br_gla, w_out, ln2_w, w_router, w_e_gate, w_e_up,
                     w_e_down, lnf_w)
    y_sample = trunk(x_sample, ln1_w, w_in, hg_lb_logits, hg_norm_w, gla_gate_a2, gla_gate_bias,
                     gla_norm_w, w_br_hg, w_br_gla, w_out, ln2_w, w_router, w_e_gate, w_e_up,
                     w_e_down, lnf_w)
    return (y_prompt, y_sample)
```

```python
import functools

import jax
import jax.numpy as jnp
from jax import lax
from jax.experimental import pallas as pl
from jax.experimental.pallas import tpu as pltpu

F32 = jnp.float32
BF16 = jnp.bfloat16

D_MODEL = 2048
CHUNK = 64
SUB = 16
HG_HEADS = 8
GLA_HEADS = 4
HEAD_DK = 128
HG_DV = 128
GLA_DV = 256
HG_WIDTH = HG_HEADS * HEAD_DK
GLA_KW = GLA_HEADS * HEAD_DK
GLA_VW = GLA_HEADS * GLA_DV
GLA_GATE_RANK = 16
GLA_GATE_TEMP = 16.0
N_EXPERTS = 16
EXPERT_FF = 2048
CAPACITY_FACTOR = 2
EPS = 1e-6
LANES = 128

COL_HQ = 0
COL_HF_F = 1024
COL_HF_B = 2048
COL_HI = 3072
COL_HG = 4096
COL_GQ = 5120
COL_GK = 5632
COL_GV = 6144
COL_GR = 7168
COL_MH = 8192
COL_MG = 10240
COL_GA = 12288
PROJ_W = 12800

VMEM_LIMIT = 58 * 1024 * 1024


def _cparams(sem):
    return pltpu.CompilerParams(dimension_semantics=sem, vmem_limit_bytes=VMEM_LIMIT)


def _sigmoid(z):
    return 1.0 / (1.0 + jnp.exp(-z))


def _silu(z):
    return z * _sigmoid(z)


def _proj_kernel(xa_ref, xb_ref, ln_ref, w_ref, o_ref, xn_ref, *, n_a):
    i = pl.program_id(0)

    def norm(x_ref):
        x = x_ref[...]
        ms = jnp.mean(x * x, axis=-1, keepdims=True)
        xn_ref[...] = (x * lax.rsqrt(ms + EPS) * ln_ref[...]).astype(BF16)

    @pl.when(jnp.logical_and(pl.program_id(1) == 0, i < n_a))
    def _():
        norm(xa_ref)

    @pl.when(jnp.logical_and(pl.program_id(1) == 0, i >= n_a))
    def _():
        norm(xb_ref)

    o_ref[...] = jnp.dot(xn_ref[...], w_ref[...], preferred_element_type=F32)


def _proj(xa, xb, ln_w, w_p, *, tm=512, tn=1280):
    na_rows, d = xa.shape
    nb_rows = xb.shape[0]
    n_a = na_rows // tm
    n_b = nb_rows // tm
    width = w_p.shape[1]
    return pl.pallas_call(
        functools.partial(_proj_kernel, n_a=n_a),
        grid=(n_a + n_b, width // tn),
        in_specs=[
            pl.BlockSpec((tm, d), lambda i, j: (jnp.minimum(i, n_a - 1), 0)),
            pl.BlockSpec((tm, d), lambda i, j: (jnp.maximum(i - n_a, 0), 0)),
            pl.BlockSpec((1, d), lambda i, j: (0, 0)),
            pl.BlockSpec((d, tn), lambda i, j: (0, j)),
        ],
        out_specs=pl.BlockSpec((tm, tn), lambda i, j: (i, j)),
        out_shape=jax.ShapeDtypeStruct((na_rows + nb_rows, width), F32),
        scratch_shapes=[pltpu.VMEM((tm, d), BF16)],
        compiler_params=_cparams(("parallel", "arbitrary")),
        name="proj",
    )(xa, xb, ln_w, w_p)


def _cumsum_rows(g, rev):
    n = g.shape[0]
    row = lax.broadcasted_iota(jnp.int32, g.shape, 0)
    x = g
    s = 1
    while s < n:
        if rev:
            x = x + jnp.where(row < n - s, pltpu.roll(x, n - s, axis=0), 0.0)
        else:
            x = x + jnp.where(row >= s, pltpu.roll(x, s, axis=0), 0.0)
        s *= 2
    return x


def _nt_dot(a, b):
    return lax.dot_general(a.astype(BF16), b.astype(BF16), (((1,), (1,)), ((), ())),
                           preferred_element_type=F32)


def _tn_dot(a, b):
    return lax.dot_general(a.astype(BF16), b.astype(BF16), (((0,), (0,)), ((), ())),
                           preferred_element_type=F32)


def _chunk_step(q, k, v, g, st, rev):
    c = CHUNK
    cum = _cumsum_rows(g, rev)
    lane = lax.broadcasted_iota(jnp.int32, (SUB, c), 1)
    blocks = []
    for blk in range(c // SUB):
        lo = blk * SUB
        q_i = q[lo:lo + SUB]
        c_i = cum[lo:lo + SUB]
        a_i = jnp.zeros((SUB, c), F32)
        for jj in range(SUB):
            j = lo + jj
            d = jnp.exp(jnp.minimum(c_i - cum[j:j + 1], 0.0))
            s = jnp.sum(q_i * d * k[j:j + 1], axis=-1, keepdims=True)
            a_i = jnp.where(lane == j, s, a_i)
        has_off = blk < c // SUB - 1 if rev else blk > 0
        if has_off:
            r = cum[lo + SUB:lo + SUB + 1] if rev else cum[lo - 1:lo]
            q_p = q_i * jnp.exp(c_i - r)
            k_p = k * jnp.exp(jnp.minimum(r - cum, 0.0))
            off = _nt_dot(q_p, k_p)
            a_i = jnp.where(lane >= lo + SUB if rev else lane < lo, off, a_i)
        blocks.append(a_i)
    a = jnp.concatenate(blocks, axis=0)
    row = lax.broadcasted_iota(jnp.int32, (c, c), 0)
    col = lax.broadcasted_iota(jnp.int32, (c, c), 1)
    a = jnp.where(col >= row if rev else col <= row, a, 0.0)
    last = cum[0:1] if rev else cum[c - 1:c]
    q_e = q * jnp.exp(cum)
    k_d = k * jnp.exp(last - cum)
    o = jnp.dot(a.astype(BF16), v.astype(BF16), preferred_element_type=F32) + _nt_dot(q_e, st)
    st_new = jnp.exp(last) * st + _tn_dot(v, k_d)
    return o, st_new


def _log_sigmoid(z):
    return jnp.minimum(z, 0.0) - jnp.log(1.0 + jnp.exp(-jnp.abs(z)))


def _hgrn_scan_kernel(qf_ref, qb_ref, zf_ref, zb_ref, vf_ref, vb_ref, lbl_ref,
                      of_ref, ob_ref, sf_ref, sb_ref, *, tb):
    @pl.when(pl.program_id(2) == 0)
    def _():
        sf_ref[...] = jnp.zeros_like(sf_ref)
        sb_ref[...] = jnp.zeros_like(sb_ref)

    l0 = lbl_ref[:, 0, :]
    l1 = lbl_ref[:, 1, :]
    m = jnp.maximum(l0, l1)
    e0 = jnp.exp(l0 - m)
    lb = e0 / (e0 + jnp.exp(l1 - m))
    n_chunks = tb // CHUNK

    def one(q_ref, z_ref, v_ref, o_ref, s_ref, low, r0, rev):
        rows = pl.ds(r0, CHUNK)
        f = low + (1.0 - low) * _sigmoid(z_ref[rows, :])
        o, st = _chunk_step(_silu(q_ref[rows, :]), 1.0 - f, v_ref[rows, :], jnp.log(f), s_ref[...], rev)
        o_ref[rows, :] = o
        s_ref[...] = st

    def body(ci, carry):
        one(qf_ref, zf_ref, vf_ref, of_ref, sf_ref, lb[0:1], pl.multiple_of(ci * CHUNK, CHUNK), False)
        one(qb_ref, zb_ref, vb_ref, ob_ref, sb_ref, lb[1:2],
            pl.multiple_of((n_chunks - 1 - ci) * CHUNK, CHUNK), True)
        return carry

    lax.fori_loop(0, n_chunks, body, 0)


def _gla_scan_kernel(qf_ref, qb_ref, kf_ref, kb_ref, vf_ref, vb_ref, af_ref, ab_ref, w2_ref, bias_ref,
                     of_ref, ob_ref, sf_ref, sb_ref, gf_ref, gb_ref, *, tb):
    @pl.when(pl.program_id(2) == 0)
    def _():
        sf_ref[...] = jnp.zeros_like(sf_ref)
        sb_ref[...] = jnp.zeros_like(sb_ref)

    def gate(a_ref, d):
        z = jnp.dot(a_ref[...].astype(BF16), w2_ref[d], preferred_element_type=F32) + bias_ref[d:d + 1, :]
        return _log_sigmoid(z) * (1.0 / GLA_GATE_TEMP)

    gf_ref[...] = gate(af_ref, 0)
    gb_ref[...] = gate(ab_ref, 1)
    n_chunks = tb // CHUNK
    scale = HEAD_DK ** -0.5

    def one(q_ref, k_ref, v_ref, g_ref, o_ref, s_ref, r0, rev):
        rows = pl.ds(r0, CHUNK)
        o, st = _chunk_step(q_ref[rows, :] * scale, k_ref[rows, :], v_ref[rows, :], g_ref[rows, :],
                            s_ref[...], rev)
        o_ref[rows, :] = o
        s_ref[...] = st

    def body(ci, carry):
        one(qf_ref, kf_ref, vf_ref, gf_ref, of_ref, sf_ref, pl.multiple_of(ci * CHUNK, CHUNK), False)
        one(qb_ref, kb_ref, vb_ref, gb_ref, ob_ref, sb_ref,
            pl.multiple_of((n_chunks - 1 - ci) * CHUNK, CHUNK), True)
        return carry

    lax.fori_loop(0, n_chunks, body, 0)


def _scan_maps(row_blk0, nb):
    def fwd(col):
        return lambda b, h, n: (row_blk0 + b * nb + n, col(h))

    def bwd(col):
        return lambda b, h, n: (row_blk0 + b * nb + nb - 1 - n, col(h))
    return fwd, bwd


def _hgrn_scan(proj, lb_logits, *, row0, batch, seq, tb):
    nb = seq // tb
    fwd, bwd = _scan_maps(row0 // tb, nb)
    ofwd, obwd = _scan_maps(0, nb)
    blk = lambda m: pl.BlockSpec((tb, HEAD_DK), m)
    cb = lambda base: (lambda h: base // HEAD_DK + h)
    n_rows = batch * seq
    out_sd = jax.ShapeDtypeStruct((n_rows, HG_HEADS * HG_DV), F32)
    return pl.pallas_call(
        functools.partial(_hgrn_scan_kernel, tb=tb),
        grid=(batch, HG_HEADS, nb),
        in_specs=[
            blk(fwd(cb(COL_HQ))), blk(bwd(cb(COL_HQ))),
            blk(fwd(cb(COL_HF_F))), blk(bwd(cb(COL_HF_B))),
            blk(fwd(cb(COL_HI))), blk(bwd(cb(COL_HI))),
            pl.BlockSpec((2, 2, HEAD_DK), lambda b, h, n: (0, 0, h)),
        ],
        out_specs=[blk(ofwd(lambda h: h)), blk(obwd(lambda h: h))],
        out_shape=[out_sd, out_sd],
        scratch_shapes=[pltpu.VMEM((HG_DV, HEAD_DK), F32), pltpu.VMEM((HG_DV, HEAD_DK), F32)],
        compiler_params=_cparams(("parallel", "parallel", "arbitrary")),
        name="hgrn_scan",
    )(proj, proj, proj, proj, proj, proj, lb_logits)


def _gla_scan(proj, w2_pad, bias, *, row0, batch, seq, tb):
    nb = seq // tb
    fwd, bwd = _scan_maps(row0 // tb, nb)
    ofwd, obwd = _scan_maps(0, nb)
    blk = lambda m: pl.BlockSpec((tb, HEAD_DK), m)
    vblk = lambda m: pl.BlockSpec((tb, GLA_DV), m)
    cb = lambda base: (lambda h: base // HEAD_DK + h)
    vb = lambda h: COL_GV // GLA_DV + h
    ga = lambda h: COL_GA // LANES
    n_rows = batch * seq
    out_sd = jax.ShapeDtypeStruct((n_rows, GLA_VW), F32)
    return pl.pallas_call(
        functools.partial(_gla_scan_kernel, tb=tb),
        grid=(batch, GLA_HEADS, nb),
        in_specs=[
            blk(fwd(cb(COL_GQ))), blk(bwd(cb(COL_GQ))),
            blk(fwd(cb(COL_GK))), blk(bwd(cb(COL_GK))),
            vblk(fwd(vb)), vblk(bwd(vb)),
            blk(fwd(ga)), blk(bwd(ga)),
            pl.BlockSpec((2, LANES, HEAD_DK), lambda b, h, n: (0, 0, h)),
            pl.BlockSpec((2, HEAD_DK), lambda b, h, n: (0, h)),
        ],
        out_specs=[vblk(ofwd(lambda h: h)), vblk(obwd(lambda h: h))],
        out_shape=[out_sd, out_sd],
        scratch_shapes=[pltpu.VMEM((GLA_DV, HEAD_DK), F32), pltpu.VMEM((GLA_DV, HEAD_DK), F32),
                        pltpu.VMEM((tb, HEAD_DK), F32), pltpu.VMEM((tb, HEAD_DK), F32)],
        compiler_params=_cparams(("parallel", "parallel", "arbitrary")),
        name="gla_scan",
    )(proj, proj, proj, proj, proj, proj, proj, proj, w2_pad, bias)


def _head_norm(o, heads, dv):
    parts = []
    for h in range(heads):
        oh = o[:, h * dv:(h + 1) * dv]
        parts.append(oh * lax.rsqrt(jnp.mean(oh * oh, axis=-1, keepdims=True) + EPS))
    return jnp.concatenate(parts, axis=1)


def _split_bf16(x):
    hi = x.astype(BF16)
    return hi, (x - hi.astype(F32)).astype(BF16)


def _merge_kernel(ohf_ref, ohb_ref, ogf_ref, ogb_ref, hg_ref, gr_ref, mh_ref, mg_ref, x_ref,
                  nwh_ref, nwg_ref, wbh_ref, wbg_ref, wo_ref, ln2_ref, wrh_ref, wrl_ref,
                  x1_ref, xn2_ref, aff_ref):
    yh = _head_norm(ohf_ref[...] + ohb_ref[...], HG_HEADS, HG_DV) * nwh_ref[...] * _silu(hg_ref[...])
    yg = _head_norm(ogf_ref[...] + ogb_ref[...], GLA_HEADS, GLA_DV) * nwg_ref[...] * _silu(gr_ref[...])
    uh = jnp.dot(yh.astype(BF16), wbh_ref[...], preferred_element_type=F32)
    ug = jnp.dot(yg.astype(BF16), wbg_ref[...], preferred_element_type=F32)
    merged = _sigmoid(mh_ref[...]) * uh + _sigmoid(mg_ref[...]) * ug
    x1 = x_ref[...] + jnp.dot(merged.astype(BF16), wo_ref[...], preferred_element_type=F32)
    x1_ref[...] = x1
    xn2 = x1 * lax.rsqrt(jnp.mean(x1 * x1, axis=-1, keepdims=True) + EPS) * ln2_ref[...]
    xn2_ref[...] = xn2
    xh, xl = _split_bf16(xn2)
    wh = wrh_ref[...]
    logits = (jnp.dot(xh, wh, preferred_element_type=F32) + jnp.dot(xl, wh, preferred_element_type=F32)
              + jnp.dot(xh, wrl_ref[...], preferred_element_type=F32))
    lane = lax.broadcasted_iota(jnp.int32, logits.shape, 1)
    lg = jnp.where(lane < N_EXPERTS, logits, -jnp.inf)
    e = jnp.exp(lg - jnp.max(lg, axis=-1, keepdims=True))
    aff_ref[...] = e / jnp.sum(e, axis=-1, keepdims=True)


def _merge(ohf, ohb, ogf, ogb, proj, x, nwh, nwg, wbh, wbg, wo, ln2, wrh, wrl, *, row0, tm):
    n, d = x.shape
    r0 = row0 // tm
    row = lambda c: (lambda i: (i, c))
    prow = lambda c: (lambda i: (r0 + i, c))
    const = lambda i: (0, 0)
    full = lambda a: pl.BlockSpec(a.shape, const, pipeline_mode=pl.Buffered(1))
    return pl.pallas_call(
        _merge_kernel,
        grid=(n // tm,),
        in_specs=[
            pl.BlockSpec((tm, HG_WIDTH), row(0)), pl.BlockSpec((tm, HG_WIDTH), row(0)),
            pl.BlockSpec((tm, GLA_VW), row(0)), pl.BlockSpec((tm, GLA_VW), row(0)),
            pl.BlockSpec((tm, HG_WIDTH), prow(COL_HG // HG_WIDTH)),
            pl.BlockSpec((tm, GLA_VW), prow(COL_GR // GLA_VW)),
            pl.BlockSpec((tm, d), prow(COL_MH // d)),
            pl.BlockSpec((tm, d), prow(COL_MG // d)),
            pl.BlockSpec((tm, d), row(0)),
            full(nwh), full(nwg), full(wbh), full(wbg), full(wo), full(ln2), full(wrh), full(wrl),
        ],
        out_specs=[pl.BlockSpec((tm, d), row(0)), pl.BlockSpec((tm, d), row(0)),
                   pl.BlockSpec((tm, LANES), row(0))],
        out_shape=[jax.ShapeDtypeStruct((n, d), F32), jax.ShapeDtypeStruct((n, d), F32),
                   jax.ShapeDtypeStruct((n, LANES), F32)],
        compiler_params=_cparams(("parallel",)),
        name="merge",
    )(ohf, ohb, ogf, ogb, proj, proj, proj, proj, x, nwh, nwg, wbh, wbg, wo, ln2, wrh, wrl)


def _route_kernel(aff_ref, idx_ref, gval_ref, pm_ref, w_ref, *, n, cap, tt):
    e_id = pl.program_id(0)

    @pl.when(e_id == 0)
    def _():
        aff = aff_ref[...]
        a_t = jnp.transpose(aff)[0:N_EXPERTS, :]
        bits = pltpu.bitcast(a_t, jnp.int32)
        thr = jnp.zeros((N_EXPERTS, 1), jnp.int32)
        for b in range(30, -1, -1):
            cand = thr | (1 << b)
            cnt = jnp.sum(jnp.where(bits >= cand, 1.0, 0.0), axis=-1, keepdims=True)
            thr = jnp.where(cnt >= cap, cand, thr)
        gt = bits > thr
        eq = bits == thr
        need = cap - jnp.sum(jnp.where(gt, 1.0, 0.0), axis=-1, keepdims=True)
        both = jnp.concatenate([jnp.where(gt, 1.0, 0.0), jnp.where(eq, 1.0, 0.0)], axis=0).astype(BF16)
        ku = lax.broadcasted_iota(jnp.int32, (LANES, LANES), 0)
        nu = lax.broadcasted_iota(jnp.int32, (LANES, LANES), 1)
        upper = jnp.where(ku <= nu, 1.0, 0.0).astype(BF16)
        carry = jnp.zeros((2 * N_EXPERTS, 1), F32)
        for blk in range(n // LANES):
            sl = slice(blk * LANES, (blk + 1) * LANES)
            p = jnp.dot(both[:, sl], upper, preferred_element_type=F32) + carry
            carry = p[:, LANES - 1:LANES]
            p_gt = p[0:N_EXPERTS]
            p_eq = p[N_EXPERTS:]
            sel = jnp.logical_or(gt[:, sl], jnp.logical_and(eq[:, sl], p_eq <= need))
            pos = p_gt + jnp.minimum(p_eq, need)
            pm_ref[:, sl] = jnp.where(sel, pos, -1.0)
        aff_hi = aff.astype(BF16)
        r1 = aff - aff_hi.astype(F32)
        aff_mid = r1.astype(BF16)
        aff_lo = (r1 - aff_mid.astype(F32)).astype(BF16)
        src = lax.broadcasted_iota(jnp.int32, (LANES, LANES), 0)
        dst = lax.broadcasted_iota(jnp.int32, (LANES, LANES), 1)
        place = lambda off: jnp.where(jnp.logical_and(src < N_EXPERTS, dst == src + off), 1.0, 0.0).astype(BF16)
        w = (jnp.dot(aff_hi, place(16), preferred_element_type=F32)
             + jnp.dot(aff_mid, place(32), preferred_element_type=F32)
             + jnp.dot(aff_lo, place(48), preferred_element_type=F32))
        tok = lax.broadcasted_iota(jnp.int32, (n, LANES), 0)
        ln = lax.broadcasted_iota(jnp.int32, (n, LANES), 1)
        w = jnp.where(ln == 0, (tok >> 7).astype(F32), w)
        w = jnp.where(ln == 1, (tok & (LANES - 1)).astype(F32), w)
        w_ref[...] = w.astype(BF16)

    def body(j, acc):
        t0 = pl.multiple_of(j * tt, tt)
        pm = pm_ref[pl.ds(e_id, 1), pl.ds(t0, tt)]
        slot = lax.broadcasted_iota(jnp.int32, (cap, tt), 0).astype(F32) + 1.0
        onehot = jnp.where(pm == slot, 1.0, 0.0).astype(BF16)
        return acc + jnp.dot(onehot, w_ref[pl.ds(t0, tt), :], preferred_element_type=F32)

    r = lax.fori_loop(0, n // tt, body, jnp.zeros((cap, LANES), F32))
    lane = lax.broadcasted_iota(jnp.int32, (cap, LANES), 1)
    idx = r[:, 0:1] * float(LANES) + r[:, 1:2]
    idx_ref[0] = idx.astype(jnp.int32)
    gmask = jnp.logical_or(jnp.logical_or(lane == e_id + 16, lane == e_id + 32), lane == e_id + 48)
    gval_ref[0] = jnp.sum(jnp.where(gmask, r, 0.0), axis=-1, keepdims=True)


def _route(aff, *, cap, tt=512):
    n = aff.shape[0]
    return pl.pallas_call(
        functools.partial(_route_kernel, n=n, cap=cap, tt=tt),
        grid=(N_EXPERTS,),
        in_specs=[pl.BlockSpec((n, LANES), lambda e: (0, 0))],
        out_specs=[pl.BlockSpec((1, cap, 1), lambda e: (e, 0, 0)),
                   pl.BlockSpec((1, cap, 1), lambda e: (e, 0, 0))],
        out_shape=[jax.ShapeDtypeStruct((N_EXPERTS, cap, 1), jnp.int32),
                   jax.ShapeDtypeStruct((N_EXPERTS, cap, 1), F32)],
        scratch_shapes=[pltpu.VMEM((N_EXPERTS, n), F32), pltpu.VMEM((n, LANES), BF16)],
        compiler_params=_cparams(("arbitrary",)),
        name="route",
    )(aff)


def _expert_kernel(idx_ref, gval_ref, xn_hbm, y_in_hbm, wg_ref, wu_ref, wd_ref, y_hbm,
                   acc_ref, ybuf_ref, xb_ref, sem_ref, *, cap, n_f):
    del y_in_hbm
    e = pl.program_id(0)
    f = pl.program_id(1)
    base = e * cap

    def x_copy(c):
        return pltpu.make_async_copy(xn_hbm.at[pl.ds(idx_ref[base + c], 1)], acc_ref.at[pl.ds(c, 1)],
                                     sem_ref.at[0])

    def y_copy(c):
        return pltpu.make_async_copy(y_hbm.at[pl.ds(idx_ref[base + c], 1)], ybuf_ref.at[pl.ds(c, 1)],
                                     sem_ref.at[1])

    def y_store(c):
        return pltpu.make_async_copy(ybuf_ref.at[pl.ds(c, 1)], y_hbm.at[pl.ds(idx_ref[base + c], 1)],
                                     sem_ref.at[2])

    def for_rows(fn):
        def body(c, carry):
            fn(c)
            return carry
        lax.fori_loop(0, cap, body, 0)

    @pl.when(f == 0)
    def _():
        for_rows(lambda c: x_copy(c).start())
        for_rows(lambda c: y_copy(c).start())
        for_rows(lambda c: x_copy(c).wait())
        xb_ref[...] = acc_ref[...].astype(BF16)

    xb = xb_ref[...]
    gate = jnp.dot(xb, wg_ref[0].astype(BF16), preferred_element_type=F32)
    up = jnp.dot(xb, wu_ref[0].astype(BF16), preferred_element_type=F32)
    h = (_silu(gate) * up).astype(BF16)
    part = jnp.dot(h, wd_ref[0].astype(BF16), preferred_element_type=F32)

    @pl.when(f == 0)
    def _():
        acc_ref[...] = part

    @pl.when(f > 0)
    def _():
        acc_ref[...] += part

    @pl.when(f == n_f - 1)
    def _():
        for_rows(lambda c: y_copy(c).wait())
        ybuf_ref[...] += acc_ref[...] * gval_ref[0]
        for_rows(lambda c: y_store(c).start())
        for_rows(lambda c: y_store(c).wait())


def _experts(idx_flat, gval, xn2, y, wg, wu, wd, *, cap, tf=256):
    n, d = xn2.shape
    ff = wg.shape[2]
    n_f = ff // tf
    grid_spec = pltpu.PrefetchScalarGridSpec(
        num_scalar_prefetch=1,
        grid=(N_EXPERTS, n_f),
        in_specs=[
            pl.BlockSpec((1, cap, 1), lambda e, f, idx: (e, 0, 0)),
            pl.BlockSpec(memory_space=pl.ANY),
            pl.BlockSpec(memory_space=pl.ANY),
            pl.BlockSpec((1, d, tf), lambda e, f, idx: (e, 0, f)),
            pl.BlockSpec((1, d, tf), lambda e, f, idx: (e, 0, f)),
            pl.BlockSpec((1, tf, d), lambda e, f, idx: (e, f, 0)),
        ],
        out_specs=pl.BlockSpec(memory_space=pl.ANY),
        scratch_shapes=[pltpu.VMEM((cap, d), F32), pltpu.VMEM((cap, d), F32), pltpu.VMEM((cap, d), BF16),
                        pltpu.SemaphoreType.DMA((3,))],
    )
    return pl.pallas_call(
        functools.partial(_expert_kernel, cap=cap, n_f=n_f),
        grid_spec=grid_spec,
        out_shape=jax.ShapeDtypeStruct((n, d), F32),
        input_output_aliases={3: 0},
        compiler_params=_cparams(("arbitrary", "arbitrary")),
        name="experts",
    )(idx_flat, gval, xn2, y, wg, wu, wd)


def _final_kernel(y_ref, w_ref, o_ref):
    y = y_ref[...]
    o_ref[...] = y * lax.rsqrt(jnp.mean(y * y, axis=-1, keepdims=True) + EPS) * w_ref[...]


def _final_norm(y, w, *, tm=512):
    n, d = y.shape
    return pl.pallas_call(
        _final_kernel,
        grid=(n // tm,),
        in_specs=[pl.BlockSpec((tm, d), lambda i: (i, 0)), pl.BlockSpec((1, d), lambda i: (0, 0))],
        out_specs=pl.BlockSpec((tm, d), lambda i: (i, 0)),
        out_shape=jax.ShapeDtypeStruct((n, d), F32),
        compiler_params=_cparams(("parallel",)),
        name="final_norm",
    )(y, w)


def _relayout_w_in(w_in):
    lo = 7168
    hi = lo + 2 * GLA_GATE_RANK
    d = w_in.shape[0]
    pad = jnp.zeros((d, PROJ_W - w_in.shape[1]), w_in.dtype)
    return jnp.concatenate([w_in[:, :lo], w_in[:, hi:], w_in[:, lo:hi], pad], axis=1).astype(BF16)


def _group(proj, x, row0, batch, seq, p, *, tb=512, tm=256):
    n = batch * seq
    cap = max(1, CAPACITY_FACTOR * n // N_EXPERTS)
    ohf, ohb = _hgrn_scan(proj, p["lb_logits"], row0=row0, batch=batch, seq=seq, tb=tb)
    ogf, ogb = _gla_scan(proj, p["w2_pad"], p["gate_bias"], row0=row0, batch=batch, seq=seq, tb=tb)
    x1, xn2, aff = _merge(ohf, ohb, ogf, ogb, proj, x, p["nwh"], p["nwg"], p["wbh"], p["wbg"], p["wo"],
                          p["ln2"], p["wrh"], p["wrl"], row0=row0, tm=tm)
    idx, gval = _route(aff, cap=cap)
    y = _experts(idx.reshape(N_EXPERTS * cap), gval, xn2, x1, p["wg"], p["wu"], p["wd"], cap=cap)
    return _final_norm(y, p["lnf"])


def kernel(x_prompt, x_sample, ln1_w, w_in, hg_lb_logits, hg_norm_w, gla_gate_a2, gla_gate_bias, gla_norm_w,
           w_br_hg, w_br_gla, w_out, ln2_w, w_router, w_e_gate, w_e_up, w_e_down, lnf_w):
    d = D_MODEL
    bp, tp, _ = x_prompt.shape
    bs, ts, _ = x_sample.shape
    xa = x_prompt.reshape(bp * tp, d)
    xb = x_sample.reshape(bs * ts, d)
    layer = 0
    a2 = gla_gate_a2[layer]
    w2_pad = jnp.zeros((2, LANES, GLA_KW), F32)
    w2_pad = w2_pad.at[0, 0:GLA_GATE_RANK].set(a2[0]).at[1, GLA_GATE_RANK:2 * GLA_GATE_RANK].set(a2[1])
    wr_pad = jnp.zeros((d, LANES), F32).at[:, :N_EXPERTS].set(w_router[layer])
    wrh = wr_pad.astype(BF16)
    p = dict(
        lb_logits=hg_lb_logits[:, layer:layer + 2, :],
        w2_pad=w2_pad.astype(BF16),
        gate_bias=gla_gate_bias[layer],
        nwh=hg_norm_w[layer].reshape(1, -1), nwg=gla_norm_w[layer].reshape(1, -1),
        wbh=w_br_hg[layer].astype(BF16), wbg=w_br_gla[layer].astype(BF16), wo=w_out[layer].astype(BF16),
        ln2=ln2_w[layer].reshape(1, -1),
        wrh=wrh, wrl=(wr_pad - wrh.astype(F32)).astype(BF16),
        wg=w_e_gate[layer], wu=w_e_up[layer], wd=w_e_down[layer],
        lnf=lnf_w.reshape(1, -1),
    )
    proj = _proj(xa, xb, ln1_w[layer].reshape(1, -1), _relayout_w_in(w_in[layer]))
    ya = _group(proj, xa, 0, bp, tp, p)
    yb = _group(proj, xb, bp * tp, bs, ts, p)
    return ya.reshape(bp, tp, d), yb.reshape(bs, ts, d)
```

```python
import functools

import jax
import jax.numpy as jnp
from jax import lax
from jax.experimental import pallas as pl
from jax.experimental.pallas import tpu as pltpu

F32 = jnp.float32
BF16 = jnp.bfloat16

D_MODEL = 2048
CHUNK = 64
SUB = 8
SCAN_HEADS = 2
HG_HEADS = 8
GLA_HEADS = 4
HEAD_DK = 128
HG_DV = 128
GLA_DV = 256
HG_WIDTH = HG_HEADS * HEAD_DK
GLA_KW = GLA_HEADS * HEAD_DK
GLA_VW = GLA_HEADS * GLA_DV
GLA_GATE_RANK = 16
GLA_GATE_TEMP = 16.0
N_EXPERTS = 16
EXPERT_FF = 2048
CAPACITY_FACTOR = 2
EPS = 1e-6
LANES = 128

COL_HQ = 0
COL_HF_F = 1024
COL_HF_B = 2048
COL_HI = 3072
COL_HG = 4096
COL_GQ = 5120
COL_GK = 5632
COL_GV = 6144
COL_GR = 7168
COL_MH = 8192
COL_MG = 10240
COL_GA = 12288
PROJ_W = 12800

VMEM_LIMIT = 58 * 1024 * 1024


def _cparams(sem):
    return pltpu.CompilerParams(dimension_semantics=sem, vmem_limit_bytes=VMEM_LIMIT)


def _sigmoid(z):
    return 0.5 * jnp.tanh(0.5 * z) + 0.5


def _silu(z):
    return z * _sigmoid(z)


def _proj_kernel(xa_ref, xb_ref, ln_ref, w_ref, o_ref, xn_ref, *, n_a):
    i = pl.program_id(0)

    def norm(x_ref):
        x = x_ref[...]
        ms = jnp.mean(x * x, axis=-1, keepdims=True)
        xn_ref[...] = (x * lax.rsqrt(ms + EPS) * ln_ref[...]).astype(BF16)

    @pl.when(jnp.logical_and(pl.program_id(1) == 0, i < n_a))
    def _():
        norm(xa_ref)

    @pl.when(jnp.logical_and(pl.program_id(1) == 0, i >= n_a))
    def _():
        norm(xb_ref)

    o_ref[...] = jnp.dot(xn_ref[...], w_ref[...], preferred_element_type=F32)


def _proj(xa, xb, ln_w, w_p, *, tm=512, tn=1280):
    na_rows, d = xa.shape
    nb_rows = xb.shape[0]
    n_a = na_rows // tm
    n_b = nb_rows // tm
    width = w_p.shape[1]
    return pl.pallas_call(
        functools.partial(_proj_kernel, n_a=n_a),
        grid=(n_a + n_b, width // tn),
        in_specs=[
            pl.BlockSpec((tm, d), lambda i, j: (jnp.minimum(i, n_a - 1), 0)),
            pl.BlockSpec((tm, d), lambda i, j: (jnp.maximum(i - n_a, 0), 0)),
            pl.BlockSpec((1, d), lambda i, j: (0, 0)),
            pl.BlockSpec((d, tn), lambda i, j: (0, j)),
        ],
        out_specs=pl.BlockSpec((tm, tn), lambda i, j: (i, j)),
        out_shape=jax.ShapeDtypeStruct((na_rows + nb_rows, width), F32),
        scratch_shapes=[pltpu.VMEM((tm, d), BF16)],
        compiler_params=_cparams(("parallel", "arbitrary")),
        name="proj",
    )(xa, xb, ln_w, w_p)


def _cumsum_rows(g, rev):
    n = g.shape[0]
    row = lax.broadcasted_iota(jnp.int32, g.shape, 0)
    x = g
    s = 1
    while s < n:
        if rev:
            x = x + jnp.where(row < n - s, pltpu.roll(x, n - s, axis=0), 0.0)
        else:
            x = x + jnp.where(row >= s, pltpu.roll(x, s, axis=0), 0.0)
        s *= 2
    return x


def _nt_dot(a, b):
    return lax.dot_general(a.astype(BF16), b.astype(BF16), (((1,), (1,)), ((), ())),
                           preferred_element_type=F32)


def _tn_dot(a, b):
    return lax.dot_general(a.astype(BF16), b.astype(BF16), (((0,), (0,)), ((), ())),
                           preferred_element_type=F32)


def _chunk_step(q, k, v, g, st, rev):
    c = CHUNK
    cum = _cumsum_rows(g, rev)
    lane = lax.broadcasted_iota(jnp.int32, (SUB, c), 1)
    blocks = []
    for blk in range(c // SUB):
        lo = blk * SUB
        q_i = q[lo:lo + SUB]
        c_i = cum[lo:lo + SUB]
        a_i = jnp.zeros((SUB, c), F32)
        for jj in range(SUB):
            j = lo + jj
            d = jnp.exp(jnp.minimum(c_i - cum[j:j + 1], 0.0))
            s = jnp.sum(q_i * d * k[j:j + 1], axis=-1, keepdims=True)
            a_i = jnp.where(lane == j, s, a_i)
        blocks.append(a_i)
    a = jnp.concatenate(blocks, axis=0)
    row = lax.broadcasted_iota(jnp.int32, (c, c), 0)
    col = lax.broadcasted_iota(jnp.int32, (c, c), 1)
    a = jnp.where(col >= row if rev else col <= row, a, 0.0)
    half = c // 2
    while half >= SUB:
        size = 2 * half
        refs = []
        for blk in range(c // size):
            r0 = blk * size + (half if rev else half - 1)
            refs.append(jnp.broadcast_to(cum[r0:r0 + 1], (size, cum.shape[1])))
        ref = refs[0] if len(refs) == 1 else jnp.concatenate(refs, axis=0)
        e_l = jnp.exp(-jnp.abs(cum - ref))
        off = _nt_dot(q * e_l, k * e_l)
        same = (row & -size) == (col & -size)
        q_late = (row & half) != 0
        k_late = (col & half) != 0
        pair = jnp.logical_and(jnp.logical_not(q_late), k_late) if rev else jnp.logical_and(
            q_late, jnp.logical_not(k_late))
        a = jnp.where(jnp.logical_and(same, pair), off, a)
        half //= 2
    last = cum[0:1] if rev else cum[c - 1:c]
    q_e = q * jnp.exp(cum)
    k_d = k * jnp.exp(last - cum)
    o = jnp.dot(a.astype(BF16), v.astype(BF16), preferred_element_type=F32) + _nt_dot(q_e, st)
    st_new = jnp.exp(last) * st + _tn_dot(v, k_d)
    return o, st_new


def _log_sigmoid(z):
    return jnp.minimum(z, 0.0) - jnp.log(1.0 + jnp.exp(-jnp.abs(z)))


def _hgrn_scan_kernel(qf_ref, qb_ref, zf_ref, zb_ref, vf_ref, vb_ref, lbl_ref,
                      of_ref, ob_ref, sf_ref, sb_ref, *, tb):
    @pl.when(pl.program_id(2) == 0)
    def _():
        sf_ref[...] = jnp.zeros_like(sf_ref)
        sb_ref[...] = jnp.zeros_like(sb_ref)

    l0 = lbl_ref[:, 0, :]
    l1 = lbl_ref[:, 1, :]
    m = jnp.maximum(l0, l1)
    e0 = jnp.exp(l0 - m)
    lb = e0 / (e0 + jnp.exp(l1 - m))
    n_chunks = tb // CHUNK

    def one(q_ref, z_ref, v_ref, o_ref, s_ref, low, r0, rev):
        rows = pl.ds(r0, CHUNK)
        for hh in range(SCAN_HEADS):
            cols = slice(hh * HEAD_DK, (hh + 1) * HEAD_DK)
            f = low[:, cols] + (1.0 - low[:, cols]) * _sigmoid(z_ref[rows, cols])
            o, st = _chunk_step(_silu(q_ref[rows, cols]), 1.0 - f, v_ref[rows, cols], jnp.log(f),
                                s_ref[hh], rev)
            o_ref[rows, cols] = o
            s_ref[hh] = st

    def body(ci, carry):
        one(qf_ref, zf_ref, vf_ref, of_ref, sf_ref, lb[0:1], pl.multiple_of(ci * CHUNK, CHUNK), False)
        one(qb_ref, zb_ref, vb_ref, ob_ref, sb_ref, lb[1:2],
            pl.multiple_of((n_chunks - 1 - ci) * CHUNK, CHUNK), True)
        return carry

    lax.fori_loop(0, n_chunks, body, 0)


def _gla_scan_kernel(qf_ref, qb_ref, kf_ref, kb_ref, vf_ref, vb_ref, af_ref, ab_ref, w2_ref, bias_ref,
                     of_ref, ob_ref, sf_ref, sb_ref, gf_ref, gb_ref, *, tb):
    @pl.when(pl.program_id(2) == 0)
    def _():
        sf_ref[...] = jnp.zeros_like(sf_ref)
        sb_ref[...] = jnp.zeros_like(sb_ref)

    def gate(a_ref, d):
        z = jnp.dot(a_ref[...].astype(BF16), w2_ref[d], preferred_element_type=F32) + bias_ref[d:d + 1, :]
        return _log_sigmoid(z) * (1.0 / GLA_GATE_TEMP)

    gf_ref[...] = gate(af_ref, 0)
    gb_ref[...] = gate(ab_ref, 1)
    n_chunks = tb // CHUNK
    scale = HEAD_DK ** -0.5

    def one(q_ref, k_ref, v_ref, g_ref, o_ref, s_ref, r0, rev):
        rows = pl.ds(r0, CHUNK)
        for hh in range(SCAN_HEADS):
            cols = slice(hh * HEAD_DK, (hh + 1) * HEAD_DK)
            vcols = slice(hh * GLA_DV, (hh + 1) * GLA_DV)
            o, st = _chunk_step(q_ref[rows, cols] * scale, k_ref[rows, cols], v_ref[rows, vcols],
                                g_ref[rows, cols], s_ref[hh], rev)
            o_ref[rows, vcols] = o
            s_ref[hh] = st

    def body(ci, carry):
        one(qf_ref, kf_ref, vf_ref, gf_ref, of_ref, sf_ref, pl.multiple_of(ci * CHUNK, CHUNK), False)
        one(qb_ref, kb_ref, vb_ref, gb_ref, ob_ref, sb_ref,
            pl.multiple_of((n_chunks - 1 - ci) * CHUNK, CHUNK), True)
        return carry

    lax.fori_loop(0, n_chunks, body, 0)


def _scan_maps(row_blk0, nb):
    def fwd(col):
        return lambda b, h, n: (row_blk0 + b * nb + n, col(h))

    def bwd(col):
        return lambda b, h, n: (row_blk0 + b * nb + nb - 1 - n, col(h))
    return fwd, bwd


def _hgrn_scan(proj, lb_logits, *, row0, batch, seq, tb):
    nb = seq // tb
    fwd, bwd = _scan_maps(row0 // tb, nb)
    ofwd, obwd = _scan_maps(0, nb)
    kw = SCAN_HEADS * HEAD_DK
    blk = lambda m: pl.BlockSpec((tb, kw), m)
    cb = lambda base: (lambda h: base // kw + h)
    n_rows = batch * seq
    out_sd = jax.ShapeDtypeStruct((n_rows, HG_HEADS * HG_DV), F32)
    st = pltpu.VMEM((SCAN_HEADS, HG_DV, HEAD_DK), F32)
    return pl.pallas_call(
        functools.partial(_hgrn_scan_kernel, tb=tb),
        grid=(batch, HG_HEADS // SCAN_HEADS, nb),
        in_specs=[
            blk(fwd(cb(COL_HQ))), blk(bwd(cb(COL_HQ))),
            blk(fwd(cb(COL_HF_F))), blk(bwd(cb(COL_HF_B))),
            blk(fwd(cb(COL_HI))), blk(bwd(cb(COL_HI))),
            pl.BlockSpec((2, 2, kw), lambda b, h, n: (0, 0, h)),
        ],
        out_specs=[blk(ofwd(lambda h: h)), blk(obwd(lambda h: h))],
        out_shape=[out_sd, out_sd],
        scratch_shapes=[st, st],
        compiler_params=_cparams(("parallel", "parallel", "arbitrary")),
        name="hgrn_scan",
    )(proj, proj, proj, proj, proj, proj, lb_logits)


def _gla_scan(proj, w2_pad, bias, *, row0, batch, seq, tb):
    nb = seq // tb
    fwd, bwd = _scan_maps(row0 // tb, nb)
    ofwd, obwd = _scan_maps(0, nb)
    kw = SCAN_HEADS * HEAD_DK
    vw = SCAN_HEADS * GLA_DV
    blk = lambda m: pl.BlockSpec((tb, kw), m)
    vblk = lambda m: pl.BlockSpec((tb, vw), m)
    ablk = lambda m: pl.BlockSpec((tb, LANES), m)
    cb = lambda base: (lambda h: base // kw + h)
    vb = lambda h: COL_GV // vw + h
    ga = lambda h: COL_GA // LANES
    st = pltpu.VMEM((SCAN_HEADS, GLA_DV, HEAD_DK), F32)
    n_rows = batch * seq
    out_sd = jax.ShapeDtypeStruct((n_rows, GLA_VW), F32)
    return pl.pallas_call(
        functools.partial(_gla_scan_kernel, tb=tb),
        grid=(batch, GLA_HEADS // SCAN_HEADS, nb),
        in_specs=[
            blk(fwd(cb(COL_GQ))), blk(bwd(cb(COL_GQ))),
            blk(fwd(cb(COL_GK))), blk(bwd(cb(COL_GK))),
            vblk(fwd(vb)), vblk(bwd(vb)),
            ablk(fwd(ga)), ablk(bwd(ga)),
            pl.BlockSpec((2, LANES, kw), lambda b, h, n: (0, 0, h)),
            pl.BlockSpec((2, kw), lambda b, h, n: (0, h)),
        ],
        out_specs=[vblk(ofwd(lambda h: h)), vblk(obwd(lambda h: h))],
        out_shape=[out_sd, out_sd],
        scratch_shapes=[st, st, pltpu.VMEM((tb, kw), F32), pltpu.VMEM((tb, kw), F32)],
        compiler_params=_cparams(("parallel", "parallel", "arbitrary")),
        name="gla_scan",
    )(proj, proj, proj, proj, proj, proj, proj, proj, w2_pad, bias)


def _head_norm(o, heads, dv):
    parts = []
    for h in range(heads):
        oh = o[:, h * dv:(h + 1) * dv]
        parts.append(oh * lax.rsqrt(jnp.mean(oh * oh, axis=-1, keepdims=True) + EPS))
    return jnp.concatenate(parts, axis=1)


def _split_bf16(x):
    hi = x.astype(BF16)
    return hi, (x - hi.astype(F32)).astype(BF16)


def _merge_kernel(ohf_ref, ohb_ref, ogf_ref, ogb_ref, hg_ref, gr_ref, mh_ref, mg_ref, x_ref,
                  nwh_ref, nwg_ref, wbh_ref, wbg_ref, wo_ref, ln2_ref, wrh_ref, wrl_ref,
                  x1_ref, xn2_ref, aff_ref):
    yh = _head_norm(ohf_ref[...] + ohb_ref[...], HG_HEADS, HG_DV) * nwh_ref[...] * _silu(hg_ref[...])
    yg = _head_norm(ogf_ref[...] + ogb_ref[...], GLA_HEADS, GLA_DV) * nwg_ref[...] * _silu(gr_ref[...])
    uh = jnp.dot(yh.astype(BF16), wbh_ref[...], preferred_element_type=F32)
    ug = jnp.dot(yg.astype(BF16), wbg_ref[...], preferred_element_type=F32)
    merged = _sigmoid(mh_ref[...]) * uh + _sigmoid(mg_ref[...]) * ug
    x1 = x_ref[...] + jnp.dot(merged.astype(BF16), wo_ref[...], preferred_element_type=F32)
    x1_ref[...] = x1
    xn2 = x1 * lax.rsqrt(jnp.mean(x1 * x1, axis=-1, keepdims=True) + EPS) * ln2_ref[...]
    xn2_ref[...] = xn2
    xh, xl = _split_bf16(xn2)
    wh = wrh_ref[...]
    logits = (jnp.dot(xh, wh, preferred_element_type=F32) + jnp.dot(xl, wh, preferred_element_type=F32)
              + jnp.dot(xh, wrl_ref[...], preferred_element_type=F32))
    lane = lax.broadcasted_iota(jnp.int32, logits.shape, 1)
    lg = jnp.where(lane < N_EXPERTS, logits, -jnp.inf)
    e = jnp.exp(lg - jnp.max(lg, axis=-1, keepdims=True))
    aff_ref[...] = e / jnp.sum(e, axis=-1, keepdims=True)


def _merge(ohf, ohb, ogf, ogb, proj, x, nwh, nwg, wbh, wbg, wo, ln2, wrh, wrl, *, row0, tm):
    n, d = x.shape
    r0 = row0 // tm
    row = lambda c: (lambda i: (i, c))
    prow = lambda c: (lambda i: (r0 + i, c))
    const = lambda i: (0, 0)
    full = lambda a: pl.BlockSpec(a.shape, const, pipeline_mode=pl.Buffered(1))
    return pl.pallas_call(
        _merge_kernel,
        grid=(n // tm,),
        in_specs=[
            pl.BlockSpec((tm, HG_WIDTH), row(0)), pl.BlockSpec((tm, HG_WIDTH), row(0)),
            pl.BlockSpec((tm, GLA_VW), row(0)), pl.BlockSpec((tm, GLA_VW), row(0)),
            pl.BlockSpec((tm, HG_WIDTH), prow(COL_HG // HG_WIDTH)),
            pl.BlockSpec((tm, GLA_VW), prow(COL_GR // GLA_VW)),
            pl.BlockSpec((tm, d), prow(COL_MH // d)),
            pl.BlockSpec((tm, d), prow(COL_MG // d)),
            pl.BlockSpec((tm, d), row(0)),
            full(nwh), full(nwg), full(wbh), full(wbg), full(wo), full(ln2), full(wrh), full(wrl),
        ],
        out_specs=[pl.BlockSpec((tm, d), row(0)), pl.BlockSpec((tm, d), row(0)),
                   pl.BlockSpec((tm, LANES), row(0))],
        out_shape=[jax.ShapeDtypeStruct((n, d), F32), jax.ShapeDtypeStruct((n, d), F32),
                   jax.ShapeDtypeStruct((n, LANES), F32)],
        compiler_params=_cparams(("parallel",)),
        name="merge",
    )(ohf, ohb, ogf, ogb, proj, proj, proj, proj, x, nwh, nwg, wbh, wbg, wo, ln2, wrh, wrl)


def _route_kernel(aff_ref, idx_ref, gval_ref, pm_ref, w_ref, *, n, cap, tt):
    e_id = pl.program_id(0)

    @pl.when(e_id == 0)
    def _():
        aff = aff_ref[...]
        a_t = jnp.transpose(aff)[0:N_EXPERTS, :]
        bits = pltpu.bitcast(a_t, jnp.int32)
        thr = jnp.zeros((N_EXPERTS, 1), jnp.int32)
        for b in range(30, -1, -1):
            cand = thr | (1 << b)
            cnt = jnp.sum(jnp.where(bits >= cand, 1.0, 0.0), axis=-1, keepdims=True)
            thr = jnp.where(cnt >= cap, cand, thr)
        gt = bits > thr
        eq = bits == thr
        need = cap - jnp.sum(jnp.where(gt, 1.0, 0.0), axis=-1, keepdims=True)
        both = jnp.concatenate([jnp.where(gt, 1.0, 0.0), jnp.where(eq, 1.0, 0.0)], axis=0).astype(BF16)
        ku = lax.broadcasted_iota(jnp.int32, (LANES, LANES), 0)
        nu = lax.broadcasted_iota(jnp.int32, (LANES, LANES), 1)
        upper = jnp.where(ku <= nu, 1.0, 0.0).astype(BF16)
        carry = jnp.zeros((2 * N_EXPERTS, 1), F32)
        for blk in range(n // LANES):
            sl = slice(blk * LANES, (blk + 1) * LANES)
            p = jnp.dot(both[:, sl], upper, preferred_element_type=F32) + carry
            carry = p[:, LANES - 1:LANES]
            p_gt = p[0:N_EXPERTS]
            p_eq = p[N_EXPERTS:]
            sel = jnp.logical_or(gt[:, sl], jnp.logical_and(eq[:, sl], p_eq <= need))
            pos = p_gt + jnp.minimum(p_eq, need)
            pm_ref[:, sl] = jnp.where(sel, pos, -1.0)
        aff_hi = aff.astype(BF16)
        r1 = aff - aff_hi.astype(F32)
        aff_mid = r1.astype(BF16)
        aff_lo = (r1 - aff_mid.astype(F32)).astype(BF16)
        src = lax.broadcasted_iota(jnp.int32, (LANES, LANES), 0)
        dst = lax.broadcasted_iota(jnp.int32, (LANES, LANES), 1)
        place = lambda off: jnp.where(jnp.logical_and(src < N_EXPERTS, dst == src + off), 1.0, 0.0).astype(BF16)
        w = (jnp.dot(aff_hi, place(16), preferred_element_type=F32)
             + jnp.dot(aff_mid, place(32), preferred_element_type=F32)
             + jnp.dot(aff_lo, place(48), preferred_element_type=F32))
        tok = lax.broadcasted_iota(jnp.int32, (n, LANES), 0)
        ln = lax.broadcasted_iota(jnp.int32, (n, LANES), 1)
        w = jnp.where(ln == 0, (tok >> 7).astype(F32), w)
        w = jnp.where(ln == 1, (tok & (LANES - 1)).astype(F32), w)
        w_ref[...] = w.astype(BF16)

    def body(j, acc):
        t0 = pl.multiple_of(j * tt, tt)
        pm = pm_ref[pl.ds(e_id, 1), pl.ds(t0, tt)]
        slot = lax.broadcasted_iota(jnp.int32, (cap, tt), 0).astype(F32) + 1.0
        onehot = jnp.where(pm == slot, 1.0, 0.0).astype(BF16)
        return acc + jnp.dot(onehot, w_ref[pl.ds(t0, tt), :], preferred_element_type=F32)

    r = lax.fori_loop(0, n // tt, body, jnp.zeros((cap, LANES), F32))
    lane = lax.broadcasted_iota(jnp.int32, (cap, LANES), 1)
    idx = r[:, 0:1] * float(LANES) + r[:, 1:2]
    idx_ref[0] = idx.astype(jnp.int32)
    gmask = jnp.logical_or(jnp.logical_or(lane == e_id + 16, lane == e_id + 32), lane == e_id + 48)
    gval_ref[0] = jnp.sum(jnp.where(gmask, r, 0.0), axis=-1, keepdims=True)


def _route(aff, *, cap, tt=512):
    n = aff.shape[0]
    return pl.pallas_call(
        functools.partial(_route_kernel, n=n, cap=cap, tt=tt),
        grid=(N_EXPERTS,),
        in_specs=[pl.BlockSpec((n, LANES), lambda e: (0, 0))],
        out_specs=[pl.BlockSpec((1, cap, 1), lambda e: (e, 0, 0)),
                   pl.BlockSpec((1, cap, 1), lambda e: (e, 0, 0))],
        out_shape=[jax.ShapeDtypeStruct((N_EXPERTS, cap, 1), jnp.int32),
                   jax.ShapeDtypeStruct((N_EXPERTS, cap, 1), F32)],
        scratch_shapes=[pltpu.VMEM((N_EXPERTS, n), F32), pltpu.VMEM((n, LANES), BF16)],
        compiler_params=_cparams(("arbitrary",)),
        name="route",
    )(aff)


def _expert_kernel(idx_ref, gval_ref, xn_hbm, y_in_hbm, wg_ref, wu_ref, wd_ref, y_hbm,
                   acc_ref, ybuf_ref, xraw_ref, xb_ref, sem_ref, *, cap, n_f):
    del y_in_hbm
    e = pl.program_id(0)
    f = pl.program_id(1)
    x_steps = n_f // 2
    y_steps = n_f - 2 - x_steps
    x_rows = cap // x_steps
    y_rows = cap // y_steps

    def x_copy(ex, c):
        return pltpu.make_async_copy(xn_hbm.at[pl.ds(idx_ref[ex * cap + c], 1)], xraw_ref.at[pl.ds(c, 1)],
                                     sem_ref.at[0])

    def y_copy(c):
        return pltpu.make_async_copy(y_hbm.at[pl.ds(idx_ref[e * cap + c], 1)], ybuf_ref.at[pl.ds(c, 1)],
                                     sem_ref.at[1])

    def y_store(c):
        return pltpu.make_async_copy(ybuf_ref.at[pl.ds(c, 1)], y_hbm.at[pl.ds(idx_ref[e * cap + c], 1)],
                                     sem_ref.at[2])

    def for_rows(fn, lo, n):
        def body(c, carry):
            fn(lo + c)
            return carry
        lax.fori_loop(0, n, body, 0, unroll=8)

    @pl.when(jnp.logical_and(f == 0, e == 0))
    def _():
        for_rows(lambda c: x_copy(0, c).start(), 0, cap)

    @pl.when(f == 0)
    def _():
        for_rows(lambda c: x_copy(e, c).wait(), 0, cap)
        xb_ref[...] = xraw_ref[...].astype(BF16)

    @pl.when(jnp.logical_and(jnp.logical_and(f >= 1, f <= x_steps), e + 1 < N_EXPERTS))
    def _():
        for_rows(lambda c: x_copy(e + 1, c).start(), (f - 1) * x_rows, x_rows)

    @pl.when(jnp.logical_and(f > x_steps, f < n_f - 1))
    def _():
        for_rows(lambda c: y_copy(c).start(), (f - 1 - x_steps) * y_rows, y_rows)

    xb = xb_ref[...]
    gate = jnp.dot(xb, wg_ref[0].astype(BF16), preferred_element_type=F32)
    up = jnp.dot(xb, wu_ref[0].astype(BF16), preferred_element_type=F32)
    h = (_silu(gate) * up).astype(BF16)
    part = jnp.dot(h, wd_ref[0].astype(BF16), preferred_element_type=F32)

    @pl.when(f == 0)
    def _():
        acc_ref[...] = part

    @pl.when(f > 0)
    def _():
        acc_ref[...] += part

    @pl.when(f == n_f - 1)
    def _():
        for_rows(lambda c: y_copy(c).wait(), 0, cap)
        ybuf_ref[...] += acc_ref[...] * gval_ref[0]
        for_rows(lambda c: y_store(c).start(), 0, cap)
        for_rows(lambda c: y_store(c).wait(), 0, cap)


def _experts(idx_flat, gval, xn2, y, wg, wu, wd, *, cap, tf=256):
    n, d = xn2.shape
    ff = wg.shape[2]
    n_f = ff // tf
    assert n_f >= 4 and cap % (n_f // 2) == 0 and cap % (n_f - 2 - n_f // 2) == 0
    grid_spec = pltpu.PrefetchScalarGridSpec(
        num_scalar_prefetch=1,
        grid=(N_EXPERTS, n_f),
        in_specs=[
            pl.BlockSpec((1, cap, 1), lambda e, f, idx: (e, 0, 0)),
            pl.BlockSpec(memory_space=pl.ANY),
            pl.BlockSpec(memory_space=pl.ANY),
            pl.BlockSpec((1, d, tf), lambda e, f, idx: (e, 0, f)),
            pl.BlockSpec((1, d, tf), lambda e, f, idx: (e, 0, f)),
            pl.BlockSpec((1, tf, d), lambda e, f, idx: (e, f, 0)),
        ],
        out_specs=pl.BlockSpec(memory_space=pl.ANY),
        scratch_shapes=[pltpu.VMEM((cap, d), F32), pltpu.VMEM((cap, d), F32), pltpu.VMEM((cap, d), F32),
                        pltpu.VMEM((cap, d), BF16), pltpu.SemaphoreType.DMA((3,))],
    )
    return pl.pallas_call(
        functools.partial(_expert_kernel, cap=cap, n_f=n_f),
        grid_spec=grid_spec,
        out_shape=jax.ShapeDtypeStruct((n, d), F32),
        input_output_aliases={3: 0},
        compiler_params=_cparams(("arbitrary", "arbitrary")),
        name="experts",
    )(idx_flat, gval, xn2, y, wg, wu, wd)


def _final_kernel(y_ref, w_ref, o_ref):
    y = y_ref[...]
    o_ref[...] = y * lax.rsqrt(jnp.mean(y * y, axis=-1, keepdims=True) + EPS) * w_ref[...]


def _final_norm(y, w, *, tm=512):
    n, d = y.shape
    return pl.pallas_call(
        _final_kernel,
        grid=(n // tm,),
        in_specs=[pl.BlockSpec((tm, d), lambda i: (i, 0)), pl.BlockSpec((1, d), lambda i: (0, 0))],
        out_specs=pl.BlockSpec((tm, d), lambda i: (i, 0)),
        out_shape=jax.ShapeDtypeStruct((n, d), F32),
        compiler_params=_cparams(("parallel",)),
        name="final_norm",
    )(y, w)


def _relayout_w_in(w_in):
    lo = 7168
    hi = lo + 2 * GLA_GATE_RANK
    d = w_in.shape[0]
    pad = jnp.zeros((d, PROJ_W - w_in.shape[1]), w_in.dtype)
    return jnp.concatenate([w_in[:, :lo], w_in[:, hi:], w_in[:, lo:hi], pad], axis=1).astype(BF16)


def _group(proj, x, row0, batch, seq, p, *, tb=512, tm=256):
    n = batch * seq
    cap = max(1, CAPACITY_FACTOR * n // N_EXPERTS)
    ohf, ohb = _hgrn_scan(proj, p["lb_logits"], row0=row0, batch=batch, seq=seq, tb=tb)
    ogf, ogb = _gla_scan(proj, p["w2_pad"], p["gate_bias"], row0=row0, batch=batch, seq=seq, tb=tb)
    x1, xn2, aff = _merge(ohf, ohb, ogf, ogb, proj, x, p["nwh"], p["nwg"], p["wbh"], p["wbg"], p["wo"],
                          p["ln2"], p["wrh"], p["wrl"], row0=row0, tm=tm)
    idx, gval = _route(aff, cap=cap)
    y = _experts(idx.reshape(N_EXPERTS * cap), gval, xn2, x1, p["wg"], p["wu"], p["wd"], cap=cap)
    return _final_norm(y, p["lnf"])


def kernel(x_prompt, x_sample, ln1_w, w_in, hg_lb_logits, hg_norm_w, gla_gate_a2, gla_gate_bias, gla_norm_w,
           w_br_hg, w_br_gla, w_out, ln2_w, w_router, w_e_gate, w_e_up, w_e_down, lnf_w):
    d = D_MODEL
    bp, tp, _ = x_prompt.shape
    bs, ts, _ = x_sample.shape
    xa = x_prompt.reshape(bp * tp, d)
    xb = x_sample.reshape(bs * ts, d)
    layer = 0
    a2 = gla_gate_a2[layer]
    w2_pad = jnp.zeros((2, LANES, GLA_KW), F32)
    w2_pad = w2_pad.at[0, 0:GLA_GATE_RANK].set(a2[0]).at[1, GLA_GATE_RANK:2 * GLA_GATE_RANK].set(a2[1])
    wr_pad = jnp.zeros((d, LANES), F32).at[:, :N_EXPERTS].set(w_router[layer])
    wrh = wr_pad.astype(BF16)
    p = dict(
        lb_logits=hg_lb_logits[:, layer:layer + 2, :],
        w2_pad=w2_pad.astype(BF16),
        gate_bias=gla_gate_bias[layer],
        nwh=hg_norm_w[layer].reshape(1, -1), nwg=gla_norm_w[layer].reshape(1, -1),
        wbh=w_br_hg[layer].astype(BF16), wbg=w_br_gla[layer].astype(BF16), wo=w_out[layer].astype(BF16),
        ln2=ln2_w[layer].reshape(1, -1),
        wrh=wrh, wrl=(wr_pad - wrh.astype(F32)).astype(BF16),
        wg=w_e_gate[layer], wu=w_e_up[layer], wd=w_e_down[layer],
        lnf=lnf_w.reshape(1, -1),
    )
    proj = _proj(xa, xb, ln1_w[layer].reshape(1, -1), _relayout_w_in(w_in[layer]))
    ya = _group(proj, xa, 0, bp, tp, p)
    yb = _group(proj, xb, bp * tp, bs, ts, p)
    return ya.reshape(bp, tp, d), yb.reshape(bs, ts, d)
```
